```python
import jax, jax.numpy as jnp
from jax import lax
import numpy as np

D_MODEL = 1024
BATCH = 2
SEQ = 8192
DEPTH = 1
DEC_BATCH = 16
DEC_SEQ = 64
PAST_LEN = 2048

CHUNK = 64
MIX_WIDTH = D_MODEL
POOL_WIDTH = D_MODEL // 2
POOL_WINDOWS = (2, 4, 8, 16)
N_POOL_GROUPS = len(POOL_WINDOWS)
POOL_GROUP = POOL_WIDTH // N_POOL_GROUPS
POOL_STATE = max(POOL_WINDOWS) - 1
N_HEADS = 4
V_HEAD_DIM = 128
QK_NOPE_DIM = 128
QK_ROPE_DIM = 64
QK_HEAD_DIM = QK_NOPE_DIM + QK_ROPE_DIM
Q_LORA = D_MODEL // 4
KV_LORA = D_MODEL // 4
ATTN_WIDTH = N_HEADS * V_HEAD_DIM
IN_COLS = POOL_WIDTH + Q_LORA + KV_LORA + QK_ROPE_DIM
ROPE_THETA = 10000.0
SOFTMAX_SCALE = QK_HEAD_DIM ** -0.5
Q_BLOCK = 128
N_EXPERTS = 32
TOP_K = 4
D_FF = D_MODEL
SWIGLU_LIMIT = 7.0
SWIGLU_ALPHA = 1.702
MOE_BLOCK = 128
PLE_DIM = 256
RMS_EPS = 1e-6

kernel_name = 'hybrid_pool_mla_moe_stream_step'


def rmsnorm(x, g):
    xf = x.astype(jnp.float32)
    y = xf * lax.rsqrt(jnp.mean(xf * xf, axis=-1, keepdims=True) + RMS_EPS)
    return (y * g.astype(jnp.float32)).astype(x.dtype)


def rope(x, pos):
    half = QK_ROPE_DIM // 2
    inv = ROPE_THETA ** (-jnp.arange(half, dtype=jnp.float32) / half)
    ang = pos.astype(jnp.float32)[:, None] * inv[None, :]
    cos = jnp.cos(ang)[None, :, None, :]
    sin = jnp.sin(ang)[None, :, None, :]
    xf = x.astype(jnp.float32)
    x1, x2 = xf[..., :half], xf[..., half:]
    return jnp.concatenate([x1 * cos - x2 * sin, x1 * sin + x2 * cos], axis=-1).astype(x.dtype)


def pool_mixer(u, prev, pos0, w_pool, pool_scale):
    B, S, C = u.shape
    P = POOL_STATE
    ext = jnp.concatenate([prev.astype(u.dtype), u], axis=1)
    extf = ext.astype(jnp.float32)
    cs = jnp.concatenate([jnp.zeros((B, 1, C), jnp.float32), jnp.cumsum(extf, axis=1)], axis=1)
    t = pos0 + jnp.arange(S)
    means = []
    for g, w in enumerate(POOL_WINDOWS):
        sl = slice(g * POOL_GROUP, (g + 1) * POOL_GROUP)
        hi = cs[:, P + 1:P + 1 + S, sl]
        lo = cs[:, P + 1 - w:P + 1 - w + S, sl]
        count = jnp.minimum(t + 1, w).astype(jnp.float32)
        means.append((hi - lo) / count[None, :, None])
    mean = jnp.stack(means, axis=2)
    d = (mean - extf[:, P:].reshape(B, S, N_POOL_GROUPS, POOL_GROUP)).astype(u.dtype)
    y = jnp.einsum('bsgc,gcd->bsgd', d, w_pool).reshape(B, S, C) * pool_scale
    return y, ext[:, -P:]


def mla_attend(q_abs, q_pe, ckv, kpe, q_pos, k_pos):
    s = (jnp.einsum('bqhc,bkc->bhqk', q_abs, ckv, preferred_element_type=jnp.float32)
         + jnp.einsum('bqhr,bkr->bhqk', q_pe, kpe, preferred_element_type=jnp.float32)) * SOFTMAX_SCALE
    visible = (k_pos[None, :] // CHUNK) <= (q_pos[:, None] // CHUNK)
    s = jnp.where(visible[None, None], s, -jnp.inf)
    p = jax.nn.softmax(s, axis=-1).astype(ckv.dtype)
    return jnp.einsum('bhqk,bkc->bqhc', p, ckv)


def mla_attention(q_abs, q_pe, ckv, kpe, q_pos):
    B, S, H, C = q_abs.shape
    k_pos = jnp.arange(ckv.shape[1])
    if S > Q_BLOCK and S % Q_BLOCK == 0:
        nb = S // Q_BLOCK

        def blk(args):
            qa, qp, qpos = args
            return mla_attend(qa, qp, ckv, kpe, qpos, k_pos)

        qa = q_abs.reshape(B, nb, Q_BLOCK, H, C).swapaxes(0, 1)
        qp = q_pe.reshape(B, nb, Q_BLOCK, H, QK_ROPE_DIM).swapaxes(0, 1)
        out = lax.map(blk, (qa, qp, q_pos.reshape(nb, Q_BLOCK)))
        return out.swapaxes(0, 1).reshape(B, S, H, C)
    return mla_attend(q_abs, q_pe, ckv, kpe, q_pos, k_pos)


def token_mixers(xn, pool_prev, ckv_prev, kpe_prev, lw):
    B, S, _ = xn.shape
    past = ckv_prev.shape[1]
    z = jnp.einsum('bsd,dc->bsc', xn, lw['w_in'])
    o1 = POOL_WIDTH
    o2 = o1 + Q_LORA
    o3 = o2 + KV_LORA
    u, cq, c_kv, k_pe = z[..., :o1], z[..., o1:o2], z[..., o2:o3], z[..., o3:]
    pool_out, pool_new = pool_mixer(u, pool_prev, past, lw['w_pool'], lw['pool_scale'])
    pos = past + jnp.arange(S)
    q = jnp.einsum('bsc,chd->bshd', rmsnorm(cq, lw['g_q']), lw['w_uq'])
    q_nope = q[..., :QK_NOPE_DIM]
    q_pe = rope(q[..., QK_NOPE_DIM:], pos)
    ckv_new = rmsnorm(c_kv, lw['g_kv'])
    kpe_new = rope(k_pe[:, :, None, :], pos)[:, :, 0, :]
    ckv_all = jnp.concatenate([ckv_prev.astype(ckv_new.dtype), ckv_new], axis=1)
    kpe_all = jnp.concatenate([kpe_prev.astype(kpe_new.dtype), kpe_new], axis=1)
    q_abs = jnp.einsum('bshd,chd->bshc', q_nope, lw['w_uk'])
    o_lat = mla_attention(q_abs, q_pe, ckv_all, kpe_all, pos)
    attn_out = jnp.einsum('bshc,chd->bshd', o_lat, lw['w_uv']).reshape(B, S, ATTN_WIDTH)
    mix = jnp.concatenate([pool_out, attn_out], axis=-1)
    return jnp.einsum('bsm,md->bsd', mix, lw['w_o']), ckv_new, kpe_new, pool_new


def moe_ffn(x, lw):
    T, D = x.shape
    logits = (jnp.einsum('td,de->te', x, lw['w_router'], preferred_element_type=jnp.float32)
              + lw['b_router'].astype(jnp.float32))
    top_val, top_idx = lax.top_k(logits, TOP_K)
    gates = jax.nn.softmax(top_val, axis=-1).astype(x.dtype)
    A = T * TOP_K
    flat_e = top_idx.reshape(A).astype(jnp.int32)
    order = jnp.argsort(flat_e)
    sorted_e = flat_e[order]
    tok = (order // TOP_K).astype(jnp.int32)
    counts = jnp.bincount(flat_e, length=N_EXPERTS).astype(jnp.int32)
    padded = (counts + MOE_BLOCK - 1) // MOE_BLOCK * MOE_BLOCK
    pad_end = jnp.cumsum(padded)
    pad_start = pad_end - padded
    start = jnp.cumsum(counts) - counts
    dest = pad_start[sorted_e] + jnp.arange(A, dtype=jnp.int32) - start[sorted_e]
    n_blocks = -(-A // MOE_BLOCK) + N_EXPERTS
    rows = n_blocks * MOE_BLOCK
    row_tok = jnp.full((rows,), T, jnp.int32).at[dest].set(tok)
    x_pad = jnp.concatenate([x, jnp.zeros((1, D), x.dtype)], axis=0)
    xb = x_pad[row_tok].reshape(n_blocks, MOE_BLOCK, D)
    block_e = jnp.minimum(jnp.searchsorted(pad_end, jnp.arange(n_blocks) * MOE_BLOCK, side='right'),
                          N_EXPERTS - 1)
    w_up, b_up, w_down, b_down = lw['w_up'], lw['b_up'], lw['w_down'], lw['b_down']

    def expert_block(args):
        xr, e = args
        h = xr @ w_up[e] + b_up[e]
        glu = jnp.minimum(h[:, :D_FF], SWIGLU_LIMIT)
        lin = jnp.clip(h[:, D_FF:], -SWIGLU_LIMIT, SWIGLU_LIMIT)
        a = glu * jax.nn.sigmoid(SWIGLU_ALPHA * glu) * (lin + 1)
        return a @ w_down[e] + b_down[e]

    yb = lax.map(expert_block, (xb, block_e)).reshape(rows, D)
    y = yb[dest] * gates.reshape(A)[order][:, None]
    return jax.ops.segment_sum(y, tok, num_segments=T)


def trunk_layer(x, p, pool_prev, ckv_prev, kpe_prev, lw):
    B, S, D = x.shape
    mix, ckv_new, kpe_new, pool_new = token_mixers(rmsnorm(x, lw['g_attn']), pool_prev, ckv_prev, kpe_prev, lw)
    x = x + mix
    x = x + moe_ffn(rmsnorm(x, lw['g_ffn']).reshape(B * S, D), lw).reshape(B, S, D)
    gate_logit = jnp.einsum('bsd,de->bse', rmsnorm(x, lw['g_ple']), lw['w_ple_gate']) + lw['b_ple_gate']
    gate = jax.nn.sigmoid(gate_logit.astype(jnp.float32)).astype(x.dtype)
    x = x + gate * jnp.einsum('bsp,pd->bsd', p, lw['w_ple'])
    return x, ckv_new, kpe_new, pool_new


def setup_inputs(seed: int = 0) -> dict:
    key = jax.random.key(seed)
    ks = iter(jax.random.split(key, 40))

    def nrm(shape, scale):
        return jax.random.normal(next(ks), shape, jnp.float32) * scale

    def gain(shape):
        return 1.0 + nrm(shape, 0.05)

    L, E = DEPTH, N_EXPERTS
    return {
        'x_prompt': nrm((BATCH, SEQ, D_MODEL), 1.0),
        'x_sample': nrm((DEC_BATCH, DEC_SEQ, D_MODEL), 1.0),
        'cache_ckv': nrm((L, DEC_BATCH, PAST_LEN, KV_LORA), 1.0),
        'cache_kpe': nrm((L, DEC_BATCH, PAST_LEN, QK_ROPE_DIM), 1.0),
        'state_pool': nrm((L, DEC_BATCH, POOL_STATE, POOL_WIDTH), 1.0),
        'p_prompt': nrm((L, BATCH, SEQ, PLE_DIM), 1.0),
        'p_sample': nrm((L, DEC_BATCH, DEC_SEQ, PLE_DIM), 1.0),
        'g_attn': gain((L, D_MODEL)),
        'w_in': nrm((L, D_MODEL, IN_COLS), D_MODEL ** -0.5),
        'w_pool': nrm((L, N_POOL_GROUPS, POOL_GROUP, POOL_GROUP), POOL_GROUP ** -0.5),
        'pool_scale': gain((L, POOL_WIDTH)),
        'g_q': gain((L, Q_LORA)),
        'w_uq': nrm((L, Q_LORA, N_HEADS, QK_HEAD_DIM), Q_LORA ** -0.5),
        'g_kv': gain((L, KV_LORA)),
        'w_uk': nrm((L, KV_LORA, N_HEADS, QK_NOPE_DIM), KV_LORA ** -0.5),
        'w_uv': nrm((L, KV_LORA, N_HEADS, V_HEAD_DIM), KV_LORA ** -0.5),
        'w_o': nrm((L, MIX_WIDTH, D_MODEL), MIX_WIDTH ** -0.5),
        'g_ffn': gain((L, D_MODEL)),
        'w_router': nrm((L, D_MODEL, E), D_MODEL ** -0.5),
        'b_router': nrm((L, E), 0.01),
        'w_up': nrm((L, E, D_MODEL, 2 * D_FF), D_MODEL ** -0.5),
        'b_up': nrm((L, E, 2 * D_FF), 0.02),
        'w_down': nrm((L, E, D_FF, D_MODEL), D_FF ** -0.5),
        'b_down': nrm((L, E, D_MODEL), 0.02),
        'g_ple': gain((L, D_MODEL)),
        'w_ple_gate': nrm((L, D_MODEL, D_MODEL), D_MODEL ** -0.5),
        'b_ple_gate': nrm((L, D_MODEL), 0.02),
        'w_ple': nrm((L, PLE_DIM, D_MODEL), PLE_DIM ** -0.5),
        'g_final': gain((D_MODEL,)),
    }


def reference(x_prompt, x_sample, cache_ckv, cache_kpe, state_pool, p_prompt, p_sample,
              g_attn, w_in, w_pool, pool_scale, g_q, w_uq, g_kv, w_uk, w_uv, w_o,
              g_ffn, w_router, b_router, w_up, b_up, w_down, b_down,
              g_ple, w_ple_gate, b_ple_gate, w_ple, g_final):
    B, S, _ = x_prompt.shape
    h_p, h_s = x_prompt, x_sample
    ckv_p_l, kpe_p_l, pool_p_l = [], [], []
    ckv_s_l, kpe_s_l, pool_s_l = [], [], []
    for i in range(DEPTH):
        lw = {'g_attn': g_attn[i], 'w_in': w_in[i], 'w_pool': w_pool[i], 'pool_scale': pool_scale[i],
              'g_q': g_q[i], 'w_uq': w_uq[i], 'g_kv': g_kv[i], 'w_uk': w_uk[i], 'w_uv': w_uv[i],
              'w_o': w_o[i], 'g_ffn': g_ffn[i], 'w_router': w_router[i], 'b_router': b_router[i],
              'w_up': w_up[i], 'b_up': b_up[i], 'w_down': w_down[i], 'b_down': b_down[i],
              'g_ple': g_ple[i], 'w_ple_gate': w_ple_gate[i], 'b_ple_gate': b_ple_gate[i],
              'w_ple': w_ple[i]}
        pool0 = jnp.zeros((B, POOL_STATE, POOL_WIDTH), x_prompt.dtype)
        ckv0 = jnp.zeros((B, 0, KV_LORA), x_prompt.dtype)
        kpe0 = jnp.zeros((B, 0, QK_ROPE_DIM), x_prompt.dtype)
        h_p, ckv_p, kpe_p, pool_p = trunk_layer(h_p, p_prompt[i], pool0, ckv0, kpe0, lw)
        h_s, ckv_s, kpe_s, pool_s = trunk_layer(h_s, p_sample[i], state_pool[i], cache_ckv[i], cache_kpe[i], lw)
        ckv_p_l.append(ckv_p)
        kpe_p_l.append(kpe_p)
        pool_p_l.append(pool_p)
        ckv_s_l.append(ckv_s)
        kpe_s_l.append(kpe_s)
        pool_s_l.append(pool_s)
    y_prompt = rmsnorm(h_p, g_final)
    y_sample = rmsnorm(h_s, g_final)
    ckv_prompt = jnp.stack(ckv_p_l)
    kpe_prompt = jnp.stack(kpe_p_l)
    pool_prompt = jnp.stack(pool_p_l)
    ckv_sample = jnp.stack(ckv_s_l)
    kpe_sample = jnp.stack(kpe_s_l)
    pool_sample = jnp.stack(pool_s_l)
    return (y_prompt, y_sample, ckv_prompt, kpe_prompt, pool_prompt, ckv_sample, kpe_sample, pool_sample)
```

```python
import functools
import math

import jax
import jax.numpy as jnp
from jax import lax
from jax.experimental import pallas as pl
from jax.experimental.pallas import tpu as pltpu

F32 = jnp.float32
BF16 = jnp.bfloat16

CHUNK = 64
POOL_WINDOWS = (2, 4, 8, 16)
POOL_GROUP = 128
POOL_WIDTH = POOL_GROUP * len(POOL_WINDOWS)
POOL_STATE = max(POOL_WINDOWS) - 1
POOL_PAD = POOL_STATE + 1
N_HEADS = 4
V_HEAD_DIM = 128
QK_NOPE_DIM = 128
QK_ROPE_DIM = 64
QK_HEAD_DIM = QK_NOPE_DIM + QK_ROPE_DIM
Q_LORA = 256
KV_LORA = 256
ROPE_THETA = 10000.0
SOFTMAX_SCALE = QK_HEAD_DIM ** -0.5
EXP2_SCALE = SOFTMAX_SCALE * math.log2(math.e)
N_EXPERTS = 32
TOP_K = 4
SWIGLU_LIMIT = 7.0
SWIGLU_ALPHA = 1.702
RMS_EPS = 1e-6
NEG_BIG = -1e30

LANES = 128
TOKEN_TILE = 512
ATTN_TILE = 512
EXPERT_TILE = 256
ROW_TILE = 256
VMEM_LIMIT = 56 * 1024 * 1024


def _rms(x, g):
    return x * lax.rsqrt(jnp.mean(x * x, axis=-1, keepdims=True) + RMS_EPS) * g


def _dot(a, b):
    return jnp.dot(a, b, preferred_element_type=F32)


def _dot_nt(a, b):
    return lax.dot_general(a, b, (((1,), (1,)), ((), ())), preferred_element_type=F32)


def _sigmoid(z):
    return 1.0 / (1.0 + jnp.exp(-z))


def _params(n_axes=1):
    return pltpu.CompilerParams(dimension_semantics=("arbitrary",) * n_axes,
                                vmem_limit_bytes=VMEM_LIMIT)


def _full(shape):
    nd = len(shape)
    return pl.BlockSpec(shape, lambda *_: (0,) * nd)


def _pre_kernel(*refs, absorbed, n_seg, seg_len, tiles_per_seq, past):
    if absorbed:
        (x_ref, cs_ref, sn_ref, state_ref, gat_ref, win_ref, wpool_ref, pscale_ref, gq_ref, wq_ref,
         gkv_ref, wukt_ref,
         ckv_ref, kpe_ref, py_ref, pst_ref, qa_ref, qp_ref, ext_ref) = refs
    else:
        (x_ref, cs_ref, sn_ref, gat_ref, win_ref, wpool_ref, pscale_ref, gq_ref, wq_ref,
         gkv_ref, wk_ref, wv_ref,
         ckv_ref, kpe_ref, py_ref, pst_ref, q_ref, k_ref, v_ref, ext_ref) = refs
    i = pl.program_id(0)
    tm = n_seg * seg_len
    stride = seg_len + POOL_PAD

    xn = _rms(x_ref[...], gat_ref[...])
    z = _dot(xn.astype(BF16), win_ref[...])
    u = z[:, :POOL_WIDTH]

    if absorbed:
        for s in range(n_seg):
            ext_ref[s * stride + 1:s * stride + POOL_PAD, :] = state_ref[s]
            ext_ref[s * stride + POOL_PAD:(s + 1) * stride, :] = u[s * seg_len:(s + 1) * seg_len]
        pos0 = past
    else:
        first = (i % tiles_per_seq) == 0

        @pl.when(first)
        def _():
            ext_ref[0:POOL_PAD, :] = jnp.zeros((POOL_PAD, POOL_WIDTH), F32)

        @pl.when(jnp.logical_not(first))
        def _():
            ext_ref[0:POOL_PAD, :] = ext_ref[tm:tm + POOL_PAD, :]

        ext_ref[POOL_PAD:POOL_PAD + tm, :] = u
        pos0 = (i % tiles_per_seq) * tm

    pos = (pos0 + lax.broadcasted_iota(jnp.int32, (seg_len, 1), 0)).astype(F32)
    cs = cs_ref[...]
    sn = sn_ref[...]
    for g, w in enumerate(POOL_WINDOWS):
        lanes = slice(g * POOL_GROUP, (g + 1) * POOL_GROUP)
        cnt = jnp.minimum(pos + 1.0, float(w))
        parts = []
        for s in range(n_seg):
            base = s * stride + POOL_PAD
            cur = ext_ref[base:base + seg_len, lanes]
            acc = cur
            for j in range(1, w):
                acc = acc + ext_ref[base - j:base - j + seg_len, lanes]
            parts.append(acc / cnt - cur)
        d = parts[0] if n_seg == 1 else jnp.concatenate(parts, axis=0)
        y = _dot(d.astype(BF16), wpool_ref[g]) * pscale_ref[:, lanes]
        py_ref[:, lanes] = y.astype(BF16)
    for s in range(n_seg):
        pst_ref[s] = ext_ref[(s + 1) * stride - POOL_STATE:(s + 1) * stride, :]

    ckv = _rms(z[:, 768:1024], gkv_ref[...])
    ckv_ref[...] = ckv
    kpe = z[:, 1024:1152] * cs + z[:, 1152:1280] * sn
    kpe_ref[...] = kpe[:, :QK_ROPE_DIM]

    cqn = _rms(z[:, 512:768], gq_ref[...]).astype(BF16)
    qz = _dot(cqn, wq_ref[...])
    if not absorbed:
        ckv_b = ckv.astype(BF16)
        kn = _dot(ckv_b, wk_ref[...])
        vv = _dot(ckv_b, wv_ref[...])
        kpe_b = kpe.astype(BF16)
    for h in range(N_HEADS):
        main = qz[:, h * 256:(h + 1) * 256]
        rot = qz[:, 1024 + h * 128:1024 + (h + 1) * 128]
        pe = main[:, 128:] * cs + rot * sn
        nope = main[:, :128].astype(BF16)
        if absorbed:
            qa_ref[h] = _dot(nope, wukt_ref[h]).astype(BF16)
            qp_ref[h] = pe[:, :QK_ROPE_DIM].astype(BF16)
        else:
            q_ref[h, :, :128] = nope
            q_ref[h, :, 128:] = pe.astype(BF16)
            k_ref[h, :, :128] = kn[:, h * 128:(h + 1) * 128].astype(BF16)
            k_ref[h, :, 128:] = kpe_b
            v_ref[h] = vv[:, h * 128:(h + 1) * 128].astype(BF16)


def _pre_call(x, cs, sn, state, w, *, absorbed, seg_len, past):
    t, d = x.shape
    tm = TOKEN_TILE
    n_seg = tm // seg_len if absorbed else 1
    if not absorbed:
        tiles_per_seq = seg_len // tm
        seg = tm
    else:
        tiles_per_seq = 1
        seg = seg_len
    n_tiles = t // tm
    row = lambda i: (i, 0)
    head = lambda i: (0, i, 0)
    if absorbed:
        cs_spec = pl.BlockSpec((tm, LANES), lambda i: (0, 0))
        n_state = t // seg_len
        in_specs = [pl.BlockSpec((tm, d), row), cs_spec, cs_spec,
                    pl.BlockSpec((n_seg, POOL_STATE, POOL_WIDTH), lambda i: (i, 0, 0))]
        args = [x, cs, sn, state]
    else:
        cs_spec = pl.BlockSpec((tm, LANES), lambda i: (i % tiles_per_seq, 0))
        n_state = t // seg_len
        in_specs = [pl.BlockSpec((tm, d), row), cs_spec, cs_spec]
        args = [x, cs, sn]
    wnames = ["g_attn", "w_in", "w_pool", "pool_scale", "g_q", "w_q", "g_kv"]
    wnames += ["w_ukt"] if absorbed else ["w_k", "w_v"]
    for n in wnames:
        in_specs.append(_full(w[n].shape))
        args.append(w[n])
    out_shape = [jax.ShapeDtypeStruct((t, KV_LORA), F32),
                 jax.ShapeDtypeStruct((t, QK_ROPE_DIM), F32),
                 jax.ShapeDtypeStruct((t, POOL_WIDTH), BF16),
                 jax.ShapeDtypeStruct((n_state, POOL_STATE, POOL_WIDTH), F32)]
    out_specs = [pl.BlockSpec((tm, KV_LORA), row),
                 pl.BlockSpec((tm, QK_ROPE_DIM), row),
                 pl.BlockSpec((tm, POOL_WIDTH), row)]
    if absorbed:
        out_specs.append(pl.BlockSpec((n_seg, POOL_STATE, POOL_WIDTH), lambda i: (i, 0, 0)))
        out_shape += [jax.ShapeDtypeStruct((N_HEADS, t, KV_LORA), BF16),
                      jax.ShapeDtypeStruct((N_HEADS, t, QK_ROPE_DIM), BF16)]
        out_specs += [pl.BlockSpec((N_HEADS, tm, KV_LORA), head),
                      pl.BlockSpec((N_HEADS, tm, QK_ROPE_DIM), head)]
    else:
        out_specs.append(pl.BlockSpec((1, POOL_STATE, POOL_WIDTH), lambda i: (i // tiles_per_seq, 0, 0)))
        out_shape += [jax.ShapeDtypeStruct((N_HEADS, t, 256), BF16),
                      jax.ShapeDtypeStruct((N_HEADS, t, 256), BF16),
                      jax.ShapeDtypeStruct((N_HEADS, t, V_HEAD_DIM), BF16)]
        out_specs += [pl.BlockSpec((N_HEADS, tm, 256), head),
                      pl.BlockSpec((N_HEADS, tm, 256), head),
                      pl.BlockSpec((N_HEADS, tm, V_HEAD_DIM), head)]
    body = functools.partial(_pre_kernel, absorbed=absorbed, n_seg=n_seg, seg_len=seg,
                             tiles_per_seq=tiles_per_seq, past=past)
    return pl.pallas_call(
        body, grid=(n_tiles,), in_specs=in_specs, out_specs=out_specs, out_shape=out_shape,
        scratch_shapes=[pltpu.VMEM((n_seg * (seg + POOL_PAD), POOL_WIDTH), F32)],
        compiler_params=_params(), name="pre_absorbed" if absorbed else "pre_prompt")(*args)


def _attn_prompt_kernel(q_ref, k_ref, v_ref, o_ref, m_ref, l_ref, acc_ref, *, tile):
    i = pl.program_id(2)
    q = q_ref[...]
    m_ref[...] = jnp.full(m_ref.shape, NEG_BIG, F32)
    l_ref[...] = jnp.zeros(l_ref.shape, F32)
    acc_ref[...] = jnp.zeros(acc_ref.shape, F32)

    def step(j, masked):
        start = pl.multiple_of(j * tile, tile)
        kj = k_ref[pl.ds(start, tile), :]
        vj = v_ref[pl.ds(start, tile), :]
        s = _dot_nt(q, kj)
        if masked:
            r = lax.broadcasted_iota(jnp.int32, (tile, tile), 0) // CHUNK
            c = lax.broadcasted_iota(jnp.int32, (tile, tile), 1) // CHUNK
            s = jnp.where(c <= r, s, NEG_BIG)
        m_prev = m_ref[...]
        m_new = jnp.maximum(m_prev, jnp.max(s, axis=-1, keepdims=True))
        alpha = jnp.exp2((m_prev - m_new) * EXP2_SCALE)
        p = jnp.exp2((s - m_new) * EXP2_SCALE)
        l_ref[...] = alpha * l_ref[...] + jnp.sum(p, axis=-1, keepdims=True)
        acc_ref[...] = alpha * acc_ref[...] + _dot(p.astype(BF16), vj)
        m_ref[...] = m_new

    def full_step(j, carry):
        step(j, False)
        return carry

    lax.fori_loop(0, i, full_step, 0)
    step(i, True)
    o_ref[...] = (acc_ref[...] / l_ref[...]).astype(BF16)


def _attn_prompt_call(q, k, v, batch, seq):
    tile = ATTN_TILE
    nq = seq // tile
    body = functools.partial(_attn_prompt_kernel, tile=tile)
    return pl.pallas_call(
        body, grid=(batch, N_HEADS, nq),
        in_specs=[pl.BlockSpec((None, tile, 256), lambda b, h, i: (h, b * nq + i, 0)),
                  pl.BlockSpec((None, seq, 256), lambda b, h, i: (h, b, 0)),
                  pl.BlockSpec((None, seq, V_HEAD_DIM), lambda b, h, i: (h, b, 0))],
        out_specs=pl.BlockSpec((tile, V_HEAD_DIM), lambda b, h, i: (b * nq + i, h)),
        out_shape=jax.ShapeDtypeStruct((batch * seq, N_HEADS * V_HEAD_DIM), BF16),
        scratch_shapes=[pltpu.VMEM((tile, 1), F32), pltpu.VMEM((tile, 1), F32),
                        pltpu.VMEM((tile, V_HEAD_DIM), F32)],
        compiler_params=_params(3), name="attn_prompt")(q, k, v)


def _attn_sample_kernel(qa_ref, qp_ref, cc_ref, ck_ref, cn_ref, kn_ref, wuv_ref, o_ref, *, seq):
    qa = jnp.concatenate([qa_ref[h] for h in range(N_HEADS)], axis=0)
    qp = jnp.concatenate([qp_ref[h] for h in range(N_HEADS)], axis=0)
    cc = cc_ref[...].astype(BF16)
    ck = ck_ref[...].astype(BF16)
    cn = cn_ref[...].astype(BF16)
    kn = kn_ref[...].astype(BF16)
    s1 = _dot_nt(qa, cc) + _dot_nt(qp, ck)
    s2 = _dot_nt(qa, cn) + _dot_nt(qp, kn)
    m = jnp.maximum(jnp.max(s1, axis=-1, keepdims=True), jnp.max(s2, axis=-1, keepdims=True))
    p1 = jnp.exp2((s1 - m) * EXP2_SCALE)
    p2 = jnp.exp2((s2 - m) * EXP2_SCALE)
    l = jnp.sum(p1, axis=-1, keepdims=True) + jnp.sum(p2, axis=-1, keepdims=True)
    o_lat = (_dot(p1.astype(BF16), cc) + _dot(p2.astype(BF16), cn)) / l
    for h in range(N_HEADS):
        oh = o_lat[h * seq:(h + 1) * seq].astype(BF16)
        o_ref[:, h * V_HEAD_DIM:(h + 1) * V_HEAD_DIM] = _dot(oh, wuv_ref[h]).astype(BF16)


def _attn_sample_call(qa, qp, cache_ckv, cache_kpe, ckv_new, kpe_new, w_uv_h, n_seq, seq):
    past = cache_ckv.shape[1]
    body = functools.partial(_attn_sample_kernel, seq=seq)
    return pl.pallas_call(
        body, grid=(n_seq,),
        in_specs=[pl.BlockSpec((N_HEADS, seq, KV_LORA), lambda i: (0, i, 0)),
                  pl.BlockSpec((N_HEADS, seq, QK_ROPE_DIM), lambda i: (0, i, 0)),
                  pl.BlockSpec((None, past, KV_LORA), lambda i: (i, 0, 0)),
                  pl.BlockSpec((None, past, QK_ROPE_DIM), lambda i: (i, 0, 0)),
                  pl.BlockSpec((seq, KV_LORA), lambda i: (i, 0)),
                  pl.BlockSpec((seq, QK_ROPE_DIM), lambda i: (i, 0)),
                  _full(w_uv_h.shape)],
        out_specs=pl.BlockSpec((seq, N_HEADS * V_HEAD_DIM), lambda i: (i, 0)),
        out_shape=jax.ShapeDtypeStruct((n_seq * seq, N_HEADS * V_HEAD_DIM), BF16),
        compiler_params=_params(), name="attn_sample")(qa, qp, cache_ckv, cache_kpe, ckv_new, kpe_new, w_uv_h)


def _post_kernel(x_ref, py_ref, at_ref, cin_ref, wo_ref, gffn_ref, wr_ref, br_ref,
                 x1_ref, xn2_ref, meta_ref, gate_ref, cout_ref, cnt_ref):
    i = pl.program_id(0)
    tm = x_ref.shape[0]

    @pl.when(i == 0)
    def _():
        cnt_ref[...] = cin_ref[...]

    mix = _dot(py_ref[...], wo_ref[:POOL_WIDTH, :]) + _dot(at_ref[...], wo_ref[POOL_WIDTH:, :])
    x1 = x_ref[...] + mix
    x1_ref[...] = x1
    xn2 = _rms(x1, gffn_ref[...])
    xn2_ref[...] = xn2
    logits = jnp.dot(xn2, wr_ref[...], preferred_element_type=F32,
                     precision=lax.Precision.HIGHEST) + br_ref[...]

    lane = lax.broadcasted_iota(jnp.int32, (tm, LANES), 1)
    lanef = lane.astype(F32)
    work = logits
    vals, idxs = [], []
    for _ in range(TOP_K):
        mx = jnp.max(work, axis=-1, keepdims=True)
        ix = jnp.min(jnp.where(work == mx, lanef, float(LANES)), axis=-1, keepdims=True)
        vals.append(mx)
        idxs.append(ix)
        work = jnp.where(lanef == ix, -3e38, work)
    es = [jnp.exp(v - vals[0]) for v in vals]
    den = es[0] + es[1] + es[2] + es[3]

    onehot = jnp.zeros((tm, LANES), F32)
    for ix in idxs:
        onehot = onehot + (lanef == ix).astype(F32)
    rr = lax.broadcasted_iota(jnp.int32, (tm, tm), 0)
    cc = lax.broadcasted_iota(jnp.int32, (tm, tm), 1)
    before = (cc < rr).astype(BF16)
    cnt = cnt_ref[...]
    seen = _dot(before, onehot.astype(BF16)) + cnt[0:1, :]

    meta = jnp.zeros((tm, LANES), F32)
    gate = jnp.zeros((tm, LANES), F32)
    for k in range(TOP_K):
        rank = jnp.sum(jnp.where(lanef == idxs[k], seen, 0.0), axis=-1, keepdims=True)
        meta = jnp.where(lane == k, rank, meta)
        meta = jnp.where(lane == TOP_K + k, idxs[k], meta)
        gate = jnp.where(lane == k, es[k] / den, gate)
    meta_ref[...] = meta.astype(jnp.int32)
    gate_ref[...] = gate
    cnt_new = cnt + jnp.sum(onehot, axis=0, keepdims=True)
    cnt_ref[...] = cnt_new
    cout_ref[...] = cnt_new


def _post_call(x, py, at, cnt_in, w):
    t, d = x.shape
    tm = TOKEN_TILE
    row = lambda i: (i, 0)
    return pl.pallas_call(
        _post_kernel, grid=(t // tm,),
        in_specs=[pl.BlockSpec((tm, d), row), pl.BlockSpec((tm, POOL_WIDTH), row),
                  pl.BlockSpec((tm, N_HEADS * V_HEAD_DIM), row), _full((8, LANES)),
                  _full(w["w_o"].shape), _full(w["g_ffn"].shape),
                  _full(w["w_router"].shape), _full(w["b_router"].shape)],
        out_specs=[pl.BlockSpec((tm, d), row), pl.BlockSpec((tm, d), row),
                   pl.BlockSpec((tm, LANES), row), pl.BlockSpec((tm, LANES), row), _full((8, LANES))],
        out_shape=[jax.ShapeDtypeStruct((t, d), F32), jax.ShapeDtypeStruct((t, d), F32),
                   jax.ShapeDtypeStruct((t, LANES), jnp.int32), jax.ShapeDtypeStruct((t, LANES), F32),
                   jax.ShapeDtypeStruct((8, LANES), F32)],
        scratch_shapes=[pltpu.VMEM((8, LANES), F32)],
        compiler_params=_params(), name="post")(x, py, at, cnt_in, w["w_o"], w["g_ffn"], w["w_router"], w["b_router"])


def _row_copy(src_ref, src_row, dst_ref, dst_row, sem):
    return pltpu.make_async_copy(src_ref.at[pl.ds(src_row, 1)], dst_ref.at[pl.ds(dst_row, 1)], sem)


def _scatter_kernel(dest_ref, xn_ref, xs_in_ref, xs_ref, sem, *, tm):
    del xs_in_ref
    base = pl.program_id(0) * tm

    def issue(t, carry):
        for k in range(TOP_K):
            _row_copy(xn_ref, base + t, xs_ref, dest_ref[(base + t) * TOP_K + k], sem).start()
        return carry

    def drain(t, carry):
        for k in range(TOP_K):
            _row_copy(xn_ref, 0, xs_ref, 0, sem).wait()
        return carry

    lax.fori_loop(0, tm, issue, 0, unroll=8)
    lax.fori_loop(0, tm, drain, 0, unroll=8)


def _scatter_call(dest, xn, xs):
    t = xn.shape[0]
    tm = ROW_TILE
    body = functools.partial(_scatter_kernel, tm=tm)
    any_spec = pl.BlockSpec(memory_space=pl.ANY)
    return pl.pallas_call(
        body,
        grid_spec=pltpu.PrefetchScalarGridSpec(
            num_scalar_prefetch=1, grid=(t // tm,), in_specs=[any_spec, any_spec], out_specs=any_spec,
            scratch_shapes=[pltpu.SemaphoreType.DMA(())]),
        out_shape=jax.ShapeDtypeStruct(xs.shape, xs.dtype),
        input_output_aliases={2: 0},
        compiler_params=_params(), name="scatter_rows")(dest, xn, xs)


def _expert_kernel(be_ref, nu_ref, xs_ref, wu_ref, bu_ref, wd_ref, bd_ref, y_ref, wub_ref, wdb_ref):
    i = pl.program_id(0)
    e = be_ref[i]
    prev = be_ref[jnp.maximum(i - 1, 0)]
    d_ff = wd_ref.shape[0]
    rows = 128

    @pl.when(jnp.logical_or(i == 0, e != prev))
    def _():
        def cast(r, carry):
            sl = pl.ds(pl.multiple_of(r * rows, rows), rows)
            wub_ref[sl, :] = wu_ref[sl, :].astype(BF16)
            wdb_ref[sl, :] = wd_ref[sl, :].astype(BF16)
            return carry
        lax.fori_loop(0, d_ff // rows, cast, 0)

    @pl.when(i < nu_ref[0])
    def _():
        x = xs_ref[...].astype(BF16)
        h = _dot(x, wub_ref[...]) + bu_ref[...]
        glu = jnp.minimum(h[:, :d_ff], SWIGLU_LIMIT)
        lin = jnp.clip(h[:, d_ff:], -SWIGLU_LIMIT, SWIGLU_LIMIT)
        a = glu * _sigmoid(SWIGLU_ALPHA * glu) * (lin + 1.0)
        y_ref[...] = _dot(a.astype(BF16), wdb_ref[...]) + bd_ref[...]

    @pl.when(i >= nu_ref[0])
    def _():
        y_ref[...] = jnp.zeros(y_ref.shape, F32)


def _expert_call(block_e, n_used, xs, w_up, b_up, w_down, b_down):
    rows, d = xs.shape
    n_exp, _, two_ff = w_up.shape
    d_ff = w_down.shape[1]
    tm = EXPERT_TILE
    blk = lambda i, be, nu: (jnp.minimum(i, nu[0] - 1), 0)
    return pl.pallas_call(
        _expert_kernel,
        grid_spec=pltpu.PrefetchScalarGridSpec(
            num_scalar_prefetch=2, grid=(rows // tm,),
            in_specs=[pl.BlockSpec((tm, d), blk),
                      pl.BlockSpec((None, d, two_ff), lambda i, be, nu: (be[i], 0, 0)),
                      pl.BlockSpec((None, 1, two_ff), lambda i, be, nu: (be[i], 0, 0)),
                      pl.BlockSpec((None, d_ff, d), lambda i, be, nu: (be[i], 0, 0)),
                      pl.BlockSpec((None, 1, d), lambda i, be, nu: (be[i], 0, 0))],
            out_specs=pl.BlockSpec((tm, d), lambda i, be, nu: (i, 0)),
            scratch_shapes=[pltpu.VMEM((d, two_ff), BF16), pltpu.VMEM((d_ff, d), BF16)]),
        out_shape=jax.ShapeDtypeStruct((rows, d), F32),
        compiler_params=_params(), name="experts")(
            block_e, n_used, xs, w_up, b_up.reshape(n_exp, 1, two_ff), w_down, b_down.reshape(n_exp, 1, d))


def _final_kernel(dest_ref, x1_ref, gate_ref, p_ref, ys_ref, gple_ref, wg_ref, bg_ref, wp_ref, gfin_ref,
                  y_ref, ybuf_ref, sem, *, tm):
    base = pl.program_id(0) * tm

    def issue(t, carry):
        for k in range(TOP_K):
            _row_copy(ys_ref, dest_ref[(base + t) * TOP_K + k], ybuf_ref.at[k], t, sem).start()
        return carry

    def drain(t, carry):
        for k in range(TOP_K):
            _row_copy(ys_ref, 0, ybuf_ref.at[k], t, sem).wait()
        return carry

    lax.fori_loop(0, tm, issue, 0, unroll=8)
    lax.fori_loop(0, tm, drain, 0, unroll=8)

    gate = gate_ref[...]
    moe = gate[:, 0:1] * ybuf_ref[0]
    for k in range(1, TOP_K):
        moe = moe + gate[:, k:k + 1] * ybuf_ref[k]
    x2 = x1_ref[...] + moe
    xn3 = _rms(x2, gple_ref[...])
    g = _sigmoid(_dot(xn3.astype(BF16), wg_ref[...]) + bg_ref[...])
    x3 = x2 + g * _dot(p_ref[...].astype(BF16), wp_ref[...])
    y_ref[...] = _rms(x3, gfin_ref[...])


def _final_call(dest, x1, gate, p, ys, w):
    t, d = x1.shape
    tm = ROW_TILE
    row = lambda i, dst: (i, 0)
    full = lambda shape: pl.BlockSpec(shape, lambda i, dst: (0,) * len(shape))
    body = functools.partial(_final_kernel, tm=tm)
    return pl.pallas_call(
        body,
        grid_spec=pltpu.PrefetchScalarGridSpec(
            num_scalar_prefetch=1, grid=(t // tm,),
            in_specs=[pl.BlockSpec((tm, d), row), pl.BlockSpec((tm, LANES), row),
                      pl.BlockSpec((tm, p.shape[1]), row), pl.BlockSpec(memory_space=pl.ANY),
                      full(w["g_ple"].shape), full(w["w_ple_gate"].shape), full(w["b_ple_gate"].shape),
                      full(w["w_ple"].shape), full(w["g_final"].shape)],
            out_specs=pl.BlockSpec((tm, d), row),
            scratch_shapes=[pltpu.VMEM((TOP_K, tm, d), F32), pltpu.SemaphoreType.DMA(())]),
        out_shape=jax.ShapeDtypeStruct((t, d), F32),
        compiler_params=_params(), name="final")(
            dest, x1, gate, p, ys, w["g_ple"], w["w_ple_gate"], w["b_ple_gate"], w["w_ple"], w["g_final"])


def _rot_cols(w):
    half = QK_ROPE_DIM // 2
    return jnp.concatenate([-w[:, half:], w[:, :half]], axis=1)


def _rope_tables(pos):
    half = QK_ROPE_DIM // 2
    inv = ROPE_THETA ** (-jnp.arange(half, dtype=F32) / half)
    ang = pos.astype(F32)[:, None] * inv[None, :]
    zeros = jnp.zeros((pos.shape[0], LANES - QK_ROPE_DIM), F32)
    cos, sin = jnp.cos(ang), jnp.sin(ang)
    return jnp.concatenate([cos, cos, zeros], axis=1), jnp.concatenate([sin, sin, zeros], axis=1)


def _prep_weights(g_attn, w_in, w_pool, pool_scale, g_q, w_uq, g_kv, w_uk, w_uv, w_o, g_ffn, w_router,
                  b_router, g_ple, w_ple_gate, b_ple_gate, w_ple, g_final):
    d = w_in.shape[0]
    o3 = POOL_WIDTH + Q_LORA + KV_LORA
    w_kpe = w_in[:, o3:o3 + QK_ROPE_DIM]
    zpad = jnp.zeros((d, LANES - QK_ROPE_DIM), F32)
    w_in_ext = jnp.concatenate([w_in, zpad, _rot_cols(w_kpe), zpad], axis=1)
    zq = jnp.zeros((Q_LORA, LANES - QK_ROPE_DIM), F32)
    mains, rots = [], []
    for h in range(N_HEADS):
        mains += [w_uq[:, h, :], zq]
        rots += [_rot_cols(w_uq[:, h, QK_NOPE_DIM:]), zq]
    w_q = jnp.concatenate(mains + rots, axis=1)
    row = lambda v: v.reshape(1, -1).astype(F32)
    n_pad = LANES - N_EXPERTS
    return {
        "g_attn": row(g_attn), "w_in": w_in_ext.astype(BF16), "w_pool": w_pool.astype(BF16),
        "pool_scale": row(pool_scale), "g_q": row(g_q), "w_q": w_q.astype(BF16), "g_kv": row(g_kv),
        "w_k": w_uk.reshape(KV_LORA, N_HEADS * QK_NOPE_DIM).astype(BF16),
        "w_v": w_uv.reshape(KV_LORA, N_HEADS * V_HEAD_DIM).astype(BF16),
        "w_ukt": jnp.transpose(w_uk, (1, 2, 0)).astype(BF16),
        "w_uv_h": jnp.transpose(w_uv, (1, 0, 2)).astype(BF16),
        "w_o": w_o.astype(BF16), "g_ffn": row(g_ffn),
        "w_router": jnp.pad(w_router.astype(F32), ((0, 0), (0, n_pad))),
        "b_router": jnp.concatenate([b_router.astype(F32), jnp.full((n_pad,), NEG_BIG, F32)]).reshape(1, LANES),
        "g_ple": row(g_ple), "w_ple_gate": w_ple_gate.astype(BF16), "b_ple_gate": row(b_ple_gate),
        "w_ple": w_ple.astype(BF16), "g_final": row(g_final),
    }


def kernel(x_prompt, x_sample, cache_ckv, cache_kpe, state_pool, p_prompt, p_sample, g_attn, w_in, w_pool, pool_scale, g_q, w_uq, g_kv, w_uk, w_uv, w_o, g_ffn, w_router, b_router, w_up, b_up, w_down, b_down, g_ple, w_ple_gate, b_ple_gate, w_ple, g_final):
    assert g_attn.shape[0] == 1, "single trunk layer"
    batch, seq, d = x_prompt.shape
    n_dec, dec_seq, _ = x_sample.shape
    past = cache_ckv.shape[2]
    assert past % CHUNK == 0 and dec_seq <= CHUNK and past >= POOL_STATE
    assert seq % ATTN_TILE == 0 and seq % TOKEN_TILE == 0 and TOKEN_TILE % dec_seq == 0
    assert dec_seq >= POOL_STATE and ATTN_TILE % CHUNK == 0
    tp, ts = batch * seq, n_dec * dec_seq
    assert ts % TOKEN_TILE == 0 and tp % ROW_TILE == 0 and ts % ROW_TILE == 0

    w = _prep_weights(g_attn[0], w_in[0], w_pool[0], pool_scale[0], g_q[0], w_uq[0], g_kv[0], w_uk[0], w_uv[0],
                      w_o[0], g_ffn[0], w_router[0], b_router[0], g_ple[0], w_ple_gate[0], b_ple_gate[0],
                      w_ple[0], g_final)
    cs_p, sn_p = _rope_tables(jnp.arange(seq))
    cs_s, sn_s = _rope_tables(past + jnp.tile(jnp.arange(dec_seq), TOKEN_TILE // dec_seq))

    xp = x_prompt.reshape(tp, d)
    xs = x_sample.reshape(ts, d)

    ckv_p, kpe_p, py_p, pst_p, q_p, k_p, v_p = _pre_call(xp, cs_p, sn_p, None, w, absorbed=False, seg_len=seq, past=0)
    at_p = _attn_prompt_call(q_p, k_p, v_p, batch, seq)
    ckv_s, kpe_s, py_s, pst_s, qa_s, qp_s = _pre_call(xs, cs_s, sn_s, state_pool[0], w, absorbed=True,
                                                       seg_len=dec_seq, past=past)
    at_s = _attn_sample_call(qa_s, qp_s, cache_ckv[0], cache_kpe[0], ckv_s, kpe_s, w["w_uv_h"], n_dec, dec_seq)

    cnt0 = jnp.zeros((8, LANES), F32)
    x1_p, xn_p, meta_p, gate_p, cnt1 = _post_call(xp, py_p, at_p, cnt0, w)
    x1_s, xn_s, meta_s, gate_s, cnt2 = _post_call(xs, py_s, at_s, cnt1, w)

    te = EXPERT_TILE
    counts = cnt2[0, :N_EXPERTS].astype(jnp.int32)
    padded = (counts + te - 1) // te * te
    pad_end = jnp.cumsum(padded)
    pad_start = pad_end - padded
    n_blocks = (tp + ts) * TOP_K // te + N_EXPERTS
    block_e = jnp.minimum(jnp.searchsorted(pad_end, jnp.arange(n_blocks) * te, side="right"),
                          N_EXPERTS - 1).astype(jnp.int32)
    n_used = (pad_end[-1:] // te).astype(jnp.int32)
    dest_p = (pad_start[meta_p[:, TOP_K:2 * TOP_K]] + meta_p[:, :TOP_K]).reshape(-1)
    dest_s = (pad_start[meta_s[:, TOP_K:2 * TOP_K]] + meta_s[:, :TOP_K]).reshape(-1)

    x_sorted = jnp.zeros((n_blocks * te, d), F32)
    x_sorted = _scatter_call(dest_p, xn_p, x_sorted)
    x_sorted = _scatter_call(dest_s, xn_s, x_sorted)
    y_sorted = _expert_call(block_e, n_used, x_sorted, w_up[0], b_up[0], w_down[0], b_down[0])

    y_p = _final_call(dest_p, x1_p, gate_p, p_prompt[0].reshape(tp, -1), y_sorted, w)
    y_s = _final_call(dest_s, x1_s, gate_s, p_sample[0].reshape(ts, -1), y_sorted, w)

    return (y_p.reshape(batch, seq, d), y_s.reshape(n_dec, dec_seq, d),
            ckv_p.reshape(1, batch, seq, KV_LORA), kpe_p.reshape(1, batch, seq, QK_ROPE_DIM),
            pst_p.reshape(1, batch, POOL_STATE, POOL_WIDTH),
            ckv_s.reshape(1, n_dec, dec_seq, KV_LORA), kpe_s.reshape(1, n_dec, dec_seq, QK_ROPE_DIM),
            pst_s.reshape(1, n_dec, POOL_STATE, POOL_WIDTH))
```

```python
import functools
import math

import jax
import jax.numpy as jnp
from jax import lax
from jax.experimental import pallas as pl
from jax.experimental.pallas import tpu as pltpu

F32 = jnp.float32
BF16 = jnp.bfloat16

CHUNK = 64
POOL_WINDOWS = (2, 4, 8, 16)
POOL_GROUP = 128
POOL_WIDTH = POOL_GROUP * len(POOL_WINDOWS)
POOL_STATE = max(POOL_WINDOWS) - 1
POOL_PAD = POOL_STATE + 1
N_HEADS = 4
V_HEAD_DIM = 128
QK_NOPE_DIM = 128
QK_ROPE_DIM = 64
QK_HEAD_DIM = QK_NOPE_DIM + QK_ROPE_DIM
Q_LORA = 256
KV_LORA = 256
ROPE_THETA = 10000.0
SOFTMAX_SCALE = QK_HEAD_DIM ** -0.5
EXP2_SCALE = SOFTMAX_SCALE * math.log2(math.e)
N_EXPERTS = 32
TOP_K = 4
SWIGLU_LIMIT = 7.0
SWIGLU_ALPHA = 1.702
RMS_EPS = 1e-6
NEG_BIG = -1e30

LANES = 128
ROW_SUBLANES = 8
TOKEN_TILE = 512
ATTN_TILE = 512
EXPERT_TILE = 256
ROW_TILE = 256
VMEM_LIMIT = 56 * 1024 * 1024


def _rms(x, g):
    return x * lax.rsqrt(jnp.mean(x * x, axis=-1, keepdims=True) + RMS_EPS) * g


def _dot(a, b):
    return jnp.dot(a, b, preferred_element_type=F32)


def _dot_nt(a, b):
    return lax.dot_general(a, b, (((1,), (1,)), ((), ())), preferred_element_type=F32)


def _sigmoid(z):
    return 1.0 / (1.0 + jnp.exp(-z))


def _params(n_axes=1):
    return pltpu.CompilerParams(dimension_semantics=("arbitrary",) * n_axes,
                                vmem_limit_bytes=VMEM_LIMIT)


def _full(shape):
    nd = len(shape)
    return pl.BlockSpec(shape, lambda *_: (0,) * nd)


def _store_row_tiles(ref, x):
    n = x.shape[0]
    for s in range(ROW_SUBLANES):
        ref[pl.ds(s, n, stride=ROW_SUBLANES), :] = x[:, s * LANES:(s + 1) * LANES]


def _load_row_tiles(ref, n):
    return jnp.concatenate([ref[pl.ds(s, n, stride=ROW_SUBLANES), :] for s in range(ROW_SUBLANES)], axis=1)


def _tile_copy(src_ref, src_row, dst_ref, dst_row, sem):
    src = src_ref.at[pl.ds(pl.multiple_of(src_row * ROW_SUBLANES, ROW_SUBLANES), ROW_SUBLANES)]
    dst = dst_ref.at[pl.ds(pl.multiple_of(dst_row * ROW_SUBLANES, ROW_SUBLANES), ROW_SUBLANES)]
    return pltpu.make_async_copy(src, dst, sem)


def _pre_kernel(*refs, absorbed, n_seg, seg_len, tiles_per_seq, past):
    if absorbed:
        (x_ref, cs_ref, sn_ref, state_ref, gat_ref, win_ref, wpool_ref, pscale_ref, gq_ref, wq_ref,
         gkv_ref, wukt_ref,
         ckv_ref, kpe_ref, py_ref, pst_ref, qa_ref, qp_ref, ext_ref) = refs
    else:
        (x_ref, cs_ref, sn_ref, gat_ref, win_ref, wpool_ref, pscale_ref, gq_ref, wq_ref,
         gkv_ref, wk_ref, wv_ref,
         ckv_ref, kpe_ref, py_ref, pst_ref, q_ref, k_ref, v_ref, ext_ref) = refs
    i = pl.program_id(0)
    tm = n_seg * seg_len
    stride = seg_len + POOL_PAD

    xn = _rms(x_ref[...], gat_ref[...])
    z = _dot(xn.astype(BF16), win_ref[...])
    u = z[:, :POOL_WIDTH]

    if absorbed:
        for s in range(n_seg):
            ext_ref[s * stride + 1:s * stride + POOL_PAD, :] = state_ref[s]
            ext_ref[s * stride + POOL_PAD:(s + 1) * stride, :] = u[s * seg_len:(s + 1) * seg_len]
        pos0 = past
    else:
        first = (i % tiles_per_seq) == 0

        @pl.when(first)
        def _():
            ext_ref[0:POOL_PAD, :] = jnp.zeros((POOL_PAD, POOL_WIDTH), F32)

        @pl.when(jnp.logical_not(first))
        def _():
            ext_ref[0:POOL_PAD, :] = ext_ref[tm:tm + POOL_PAD, :]

        ext_ref[POOL_PAD:POOL_PAD + tm, :] = u
        pos0 = (i % tiles_per_seq) * tm

    pos = (pos0 + lax.broadcasted_iota(jnp.int32, (seg_len, 1), 0)).astype(F32)
    cs = cs_ref[...]
    sn = sn_ref[...]
    for g, w in enumerate(POOL_WINDOWS):
        lanes = slice(g * POOL_GROUP, (g + 1) * POOL_GROUP)
        cnt = jnp.minimum(pos + 1.0, float(w))
        parts = []
        for s in range(n_seg):
            base = s * stride + POOL_PAD
            cur = ext_ref[base:base + seg_len, lanes]
            acc = cur
            for j in range(1, w):
                acc = acc + ext_ref[base - j:base - j + seg_len, lanes]
            parts.append(acc / cnt - cur)
        d = parts[0] if n_seg == 1 else jnp.concatenate(parts, axis=0)
        y = _dot(d.astype(BF16), wpool_ref[g]) * pscale_ref[:, lanes]
        py_ref[:, lanes] = y.astype(BF16)
    for s in range(n_seg):
        pst_ref[s] = ext_ref[(s + 1) * stride - POOL_STATE:(s + 1) * stride, :]

    ckv = _rms(z[:, 768:1024], gkv_ref[...])
    ckv_ref[...] = ckv
    kpe = z[:, 1024:1152] * cs + z[:, 1152:1280] * sn
    kpe_ref[...] = kpe[:, :QK_ROPE_DIM]

    cqn = _rms(z[:, 512:768], gq_ref[...]).astype(BF16)
    qz = _dot(cqn, wq_ref[...])
    if not absorbed:
        ckv_b = ckv.astype(BF16)
        kn = _dot(ckv_b, wk_ref[...])
        vvt = _dot_nt(wv_ref[...], ckv_b)
        kpe_b = kpe.astype(BF16)
    for h in range(N_HEADS):
        main = qz[:, h * 256:(h + 1) * 256]
        rot = qz[:, 1024 + h * 128:1024 + (h + 1) * 128]
        pe = main[:, 128:] * cs + rot * sn
        nope = main[:, :128].astype(BF16)
        if absorbed:
            qa_ref[h] = _dot(nope, wukt_ref[h]).astype(BF16)
            qp_ref[h] = pe[:, :QK_ROPE_DIM].astype(BF16)
        else:
            q_ref[h, :, :128] = nope
            q_ref[h, :, 128:] = pe.astype(BF16)
            k_ref[h, :, :128] = kn[:, h * 128:(h + 1) * 128].astype(BF16)
            k_ref[h, :, 128:] = kpe_b
            v_ref[h, 0] = vvt[h * 128:(h + 1) * 128, :].astype(BF16)


def _pre_call(x, cs, sn, state, w, *, absorbed, seg_len, past):
    t, d = x.shape
    tm = TOKEN_TILE
    n_seg = tm // seg_len if absorbed else 1
    if not absorbed:
        tiles_per_seq = seg_len // tm
        seg = tm
    else:
        tiles_per_seq = 1
        seg = seg_len
    n_tiles = t // tm
    row = lambda i: (i, 0)
    head = lambda i: (0, i, 0)
    if absorbed:
        cs_spec = pl.BlockSpec((tm, LANES), lambda i: (0, 0))
        n_state = t // seg_len
        in_specs = [pl.BlockSpec((tm, d), row), cs_spec, cs_spec,
                    pl.BlockSpec((n_seg, POOL_STATE, POOL_WIDTH), lambda i: (i, 0, 0))]
        args = [x, cs, sn, state]
    else:
        cs_spec = pl.BlockSpec((tm, LANES), lambda i: (i % tiles_per_seq, 0))
        n_state = t // seg_len
        in_specs = [pl.BlockSpec((tm, d), row), cs_spec, cs_spec]
        args = [x, cs, sn]
    wnames = ["g_attn", "w_in", "w_pool", "pool_scale", "g_q", "w_q", "g_kv"]
    wnames += ["w_ukt"] if absorbed else ["w_k", "w_v"]
    for n in wnames:
        in_specs.append(_full(w[n].shape))
        args.append(w[n])
    out_shape = [jax.ShapeDtypeStruct((t, KV_LORA), F32),
                 jax.ShapeDtypeStruct((t, QK_ROPE_DIM), F32),
                 jax.ShapeDtypeStruct((t, POOL_WIDTH), BF16),
                 jax.ShapeDtypeStruct((n_state, POOL_STATE, POOL_WIDTH), F32)]
    out_specs = [pl.BlockSpec((tm, KV_LORA), row),
                 pl.BlockSpec((tm, QK_ROPE_DIM), row),
                 pl.BlockSpec((tm, POOL_WIDTH), row)]
    if absorbed:
        out_specs.append(pl.BlockSpec((n_seg, POOL_STATE, POOL_WIDTH), lambda i: (i, 0, 0)))
        out_shape += [jax.ShapeDtypeStruct((N_HEADS, t, KV_LORA), BF16),
                      jax.ShapeDtypeStruct((N_HEADS, t, QK_ROPE_DIM), BF16)]
        out_specs += [pl.BlockSpec((N_HEADS, tm, KV_LORA), head),
                      pl.BlockSpec((N_HEADS, tm, QK_ROPE_DIM), head)]
    else:
        out_specs.append(pl.BlockSpec((1, POOL_STATE, POOL_WIDTH), lambda i: (i // tiles_per_seq, 0, 0)))
        out_shape += [jax.ShapeDtypeStruct((N_HEADS, t, 256), BF16),
                      jax.ShapeDtypeStruct((N_HEADS, t, 256), BF16),
                      jax.ShapeDtypeStruct((N_HEADS, n_tiles, V_HEAD_DIM, tm), BF16)]
        out_specs += [pl.BlockSpec((N_HEADS, tm, 256), head),
                      pl.BlockSpec((N_HEADS, tm, 256), head),
                      pl.BlockSpec((N_HEADS, 1, V_HEAD_DIM, tm), lambda i: (0, i, 0, 0))]
    body = functools.partial(_pre_kernel, absorbed=absorbed, n_seg=n_seg, seg_len=seg,
                             tiles_per_seq=tiles_per_seq, past=past)
    return pl.pallas_call(
        body, grid=(n_tiles,), in_specs=in_specs, out_specs=out_specs, out_shape=out_shape,
        scratch_shapes=[pltpu.VMEM((n_seg * (seg + POOL_PAD), POOL_WIDTH), F32)],
        compiler_params=_params(), name="pre_absorbed" if absorbed else "pre_prompt")(*args)


def _attn_prompt_kernel(q_ref, k_ref, vt_ref, o_ref, m_ref, l_ref, acc_ref, sa_ref, sb_ref, *, tile):
    i = pl.program_id(2)
    q = q_ref[...]
    m_ref[...] = jnp.full(m_ref.shape, NEG_BIG, F32)
    l_ref[...] = jnp.zeros(l_ref.shape, F32)
    acc_ref[...] = jnp.zeros(acc_ref.shape, F32)

    def scores(j):
        start = pl.multiple_of(j * tile, tile)
        return _dot_nt(k_ref[pl.ds(start, tile), :], q)

    def absorb(s, j, masked):
        if masked:
            kc = lax.broadcasted_iota(jnp.int32, (tile, tile), 0) // CHUNK
            qc = lax.broadcasted_iota(jnp.int32, (tile, tile), 1) // CHUNK
            s = jnp.where(kc <= qc, s, NEG_BIG)
        m_prev = m_ref[...]
        m_new = jnp.maximum(m_prev, jnp.max(s, axis=0, keepdims=True))
        alpha = jnp.exp2((m_prev - m_new) * EXP2_SCALE)
        p = jnp.exp2((s - m_new) * EXP2_SCALE)
        l_ref[...] = alpha * l_ref[...] + jnp.sum(p, axis=0, keepdims=True)
        acc_ref[...] = alpha * acc_ref[...] + _dot(vt_ref[j], p.astype(BF16))
        m_ref[...] = m_new

    sa_ref[...] = scores(0)

    def pipelined(j, carry):
        @pl.when(j % 2 == 0)
        def _():
            sb_ref[...] = scores(j + 1)
            absorb(sa_ref[...], j, False)

        @pl.when(j % 2 == 1)
        def _():
            sa_ref[...] = scores(j + 1)
            absorb(sb_ref[...], j, False)
        return carry

    lax.fori_loop(0, i, pipelined, 0)

    @pl.when(i % 2 == 0)
    def _():
        absorb(sa_ref[...], i, True)

    @pl.when(i % 2 == 1)
    def _():
        absorb(sb_ref[...], i, True)

    o_ref[...] = (acc_ref[...] / l_ref[...]).T.astype(BF16)


def _attn_prompt_call(q, k, vt, batch, seq):
    tile = ATTN_TILE
    nq = seq // tile
    body = functools.partial(_attn_prompt_kernel, tile=tile)
    return pl.pallas_call(
        body, grid=(batch, N_HEADS, nq),
        in_specs=[pl.BlockSpec((None, tile, 256), lambda b, h, i: (h, b * nq + i, 0)),
                  pl.BlockSpec((None, seq, 256), lambda b, h, i: (h, b, 0)),
                  pl.BlockSpec((None, nq, V_HEAD_DIM, tile), lambda b, h, i: (h, b, 0, 0))],
        out_specs=pl.BlockSpec((tile, V_HEAD_DIM), lambda b, h, i: (b * nq + i, h)),
        out_shape=jax.ShapeDtypeStruct((batch * seq, N_HEADS * V_HEAD_DIM), BF16),
        scratch_shapes=[pltpu.VMEM((1, tile), F32), pltpu.VMEM((1, tile), F32),
                        pltpu.VMEM((V_HEAD_DIM, tile), F32),
                        pltpu.VMEM((tile, tile), F32), pltpu.VMEM((tile, tile), F32)],
        compiler_params=_params(3), name="attn_prompt")(q, k, vt)


def _attn_sample_kernel(qa_ref, qp_ref, cc_ref, ck_ref, cn_ref, kn_ref, wuv_ref, o_ref, *, seq):
    qa = jnp.concatenate([qa_ref[h] for h in range(N_HEADS)], axis=0)
    qp = jnp.concatenate([qp_ref[h] for h in range(N_HEADS)], axis=0)
    cc = cc_ref[...].astype(BF16)
    ck = ck_ref[...].astype(BF16)
    cn = cn_ref[...].astype(BF16)
    kn = kn_ref[...].astype(BF16)
    s1 = _dot_nt(qa, cc) + _dot_nt(qp, ck)
    s2 = _dot_nt(qa, cn) + _dot_nt(qp, kn)
    m = jnp.maximum(jnp.max(s1, axis=-1, keepdims=True), jnp.max(s2, axis=-1, keepdims=True))
    p1 = jnp.exp2((s1 - m) * EXP2_SCALE)
    p2 = jnp.exp2((s2 - m) * EXP2_SCALE)
    l = jnp.sum(p1, axis=-1, keepdims=True) + jnp.sum(p2, axis=-1, keepdims=True)
    o_lat = (_dot(p1.astype(BF16), cc) + _dot(p2.astype(BF16), cn)) / l
    for h in range(N_HEADS):
        oh = o_lat[h * seq:(h + 1) * seq].astype(BF16)
        o_ref[:, h * V_HEAD_DIM:(h + 1) * V_HEAD_DIM] = _dot(oh, wuv_ref[h]).astype(BF16)


def _attn_sample_call(qa, qp, cache_ckv, cache_kpe, ckv_new, kpe_new, w_uv_h, n_seq, seq):
    past = cache_ckv.shape[1]
    body = functools.partial(_attn_sample_kernel, seq=seq)
    return pl.pallas_call(
        body, grid=(n_seq,),
        in_specs=[pl.BlockSpec((N_HEADS, seq, KV_LORA), lambda i: (0, i, 0)),
                  pl.BlockSpec((N_HEADS, seq, QK_ROPE_DIM), lambda i: (0, i, 0)),
                  pl.BlockSpec((None, past, KV_LORA), lambda i: (i, 0, 0)),
                  pl.BlockSpec((None, past, QK_ROPE_DIM), lambda i: (i, 0, 0)),
                  pl.BlockSpec((seq, KV_LORA), lambda i: (i, 0)),
                  pl.BlockSpec((seq, QK_ROPE_DIM), lambda i: (i, 0)),
                  _full(w_uv_h.shape)],
        out_specs=pl.BlockSpec((seq, N_HEADS * V_HEAD_DIM), lambda i: (i, 0)),
        out_shape=jax.ShapeDtypeStruct((n_seq * seq, N_HEADS * V_HEAD_DIM), BF16),
        compiler_params=_params(), name="attn_sample")(qa, qp, cache_ckv, cache_kpe, ckv_new, kpe_new, w_uv_h)


def _post_kernel(x_ref, py_ref, at_ref, cin_ref, wo_ref, gffn_ref, wr_ref, br_ref,
                 x1_ref, xn2_ref, meta_ref, gate_ref, cout_ref, cnt_ref):
    i = pl.program_id(0)
    tm = x_ref.shape[0]

    @pl.when(i == 0)
    def _():
        cnt_ref[...] = cin_ref[...]

    mix = _dot(py_ref[...], wo_ref[:POOL_WIDTH, :]) + _dot(at_ref[...], wo_ref[POOL_WIDTH:, :])
    x1 = x_ref[...] + mix
    x1_ref[...] = x1
    xn2 = _rms(x1, gffn_ref[...])
    _store_row_tiles(xn2_ref, xn2)
    logits = jnp.dot(xn2, wr_ref[...], preferred_element_type=F32,
                     precision=lax.Precision.HIGHEST) + br_ref[...]

    lane = lax.broadcasted_iota(jnp.int32, (tm, LANES), 1)
    lanef = lane.astype(F32)
    work = logits
    vals, idxs = [], []
    for _ in range(TOP_K):
        mx = jnp.max(work, axis=-1, keepdims=True)
        ix = jnp.min(jnp.where(work == mx, lanef, float(LANES)), axis=-1, keepdims=True)
        vals.append(mx)
        idxs.append(ix)
        work = jnp.where(lanef == ix, -3e38, work)
    es = [jnp.exp(v - vals[0]) for v in vals]
    den = es[0] + es[1] + es[2] + es[3]

    onehot = jnp.zeros((tm, LANES), F32)
    for ix in idxs:
        onehot = onehot + (lanef == ix).astype(F32)
    rr = lax.broadcasted_iota(jnp.int32, (tm, tm), 0)
    cc = lax.broadcasted_iota(jnp.int32, (tm, tm), 1)
    before = (cc < rr).astype(BF16)
    cnt = cnt_ref[...]
    seen = _dot(before, onehot.astype(BF16)) + cnt[0:1, :]

    meta = jnp.zeros((tm, LANES), F32)
    gate = jnp.zeros((tm, LANES), F32)
    for k in range(TOP_K):
        rank = jnp.sum(jnp.where(lanef == idxs[k], seen, 0.0), axis=-1, keepdims=True)
        meta = jnp.where(lane == k, rank, meta)
        meta = jnp.where(lane == TOP_K + k, idxs[k], meta)
        gate = jnp.where(lane == k, es[k] / den, gate)
    meta_ref[...] = meta.astype(jnp.int32)
    gate_ref[...] = gate
    cnt_new = cnt + jnp.sum(onehot, axis=0, keepdims=True)
    cnt_ref[...] = cnt_new
    cout_ref[...] = cnt_new


def _post_call(x, py, at, cnt_in, w):
    t, d = x.shape
    tm = TOKEN_TILE
    row = lambda i: (i, 0)
    return pl.pallas_call(
        _post_kernel, grid=(t // tm,),
        in_specs=[pl.BlockSpec((tm, d), row), pl.BlockSpec((tm, POOL_WIDTH), row),
                  pl.BlockSpec((tm, N_HEADS * V_HEAD_DIM), row), _full((8, LANES)),
                  _full(w["w_o"].shape), _full(w["g_ffn"].shape),
                  _full(w["w_router"].shape), _full(w["b_router"].shape)],
        out_specs=[pl.BlockSpec((tm, d), row), pl.BlockSpec((tm * ROW_SUBLANES, LANES), row),
                   pl.BlockSpec((tm, LANES), row), pl.BlockSpec((tm, LANES), row), _full((8, LANES))],
        out_shape=[jax.ShapeDtypeStruct((t, d), F32), jax.ShapeDtypeStruct((t * ROW_SUBLANES, LANES), F32),
                   jax.ShapeDtypeStruct((t, LANES), jnp.int32), jax.ShapeDtypeStruct((t, LANES), F32),
                   jax.ShapeDtypeStruct((8, LANES), F32)],
        scratch_shapes=[pltpu.VMEM((8, LANES), F32)],
        compiler_params=_params(), name="post")(x, py, at, cnt_in, w["w_o"], w["g_ffn"], w["w_router"], w["b_router"])


def _scatter_kernel(dest_ref, xn_ref, xs_in_ref, xs_ref, sem, *, tm):
    del xs_in_ref
    base = pl.program_id(0) * tm

    def issue(t, carry):
        for k in range(TOP_K):
            _tile_copy(xn_ref, t, xs_ref, dest_ref[(base + t) * TOP_K + k], sem).start()
        return carry

    def drain(t, carry):
        for k in range(TOP_K):
            _tile_copy(xn_ref, t, xs_ref, 0, sem).wait()
        return carry

    lax.fori_loop(0, tm, issue, 0, unroll=8)
    lax.fori_loop(0, tm, drain, 0, unroll=8)


def _scatter_call(dest, xn, xs):
    tm = ROW_TILE
    n_tiles = xn.shape[0] // (tm * ROW_SUBLANES)
    body = functools.partial(_scatter_kernel, tm=tm)
    any_spec = pl.BlockSpec(memory_space=pl.ANY)
    return pl.pallas_call(
        body,
        grid_spec=pltpu.PrefetchScalarGridSpec(
            num_scalar_prefetch=1, grid=(n_tiles,),
            in_specs=[pl.BlockSpec((tm * ROW_SUBLANES, LANES), lambda i, dst: (i, 0)), any_spec],
            out_specs=any_spec,
            scratch_shapes=[pltpu.SemaphoreType.DMA(())]),
        out_shape=jax.ShapeDtypeStruct(xs.shape, xs.dtype),
        input_output_aliases={2: 0},
        compiler_params=_params(), name="scatter_rows")(dest, xn, xs)


def _expert_kernel(be_ref, nu_ref, xs_ref, wu_ref, bu_ref, wd_ref, bd_ref, y_ref, wub_ref, wdb_ref):
    i = pl.program_id(0)
    e = be_ref[i]
    prev = be_ref[jnp.maximum(i - 1, 0)]
    d_ff = wd_ref.shape[0]
    tm = xs_ref.shape[0] // ROW_SUBLANES
    rows = 128

    @pl.when(jnp.logical_or(i == 0, e != prev))
    def _():
        def cast(r, carry):
            sl = pl.ds(pl.multiple_of(r * rows, rows), rows)
            wub_ref[sl, :] = wu_ref[sl, :].astype(BF16)
            wdb_ref[sl, :] = wd_ref[sl, :].astype(BF16)
            return carry
        lax.fori_loop(0, d_ff // rows, cast, 0)

    @pl.when(i < nu_ref[0])
    def _():
        x = _load_row_tiles(xs_ref, tm).astype(BF16)
        h = _dot(x, wub_ref[...]) + bu_ref[...]
        glu = jnp.minimum(h[:, :d_ff], SWIGLU_LIMIT)
        lin = jnp.clip(h[:, d_ff:], -SWIGLU_LIMIT, SWIGLU_LIMIT)
        a = glu * _sigmoid(SWIGLU_ALPHA * glu) * (lin + 1.0)
        _store_row_tiles(y_ref, _dot(a.astype(BF16), wdb_ref[...]) + bd_ref[...])

    @pl.when(i >= nu_ref[0])
    def _():
        y_ref[...] = jnp.zeros(y_ref.shape, F32)


def _expert_call(block_e, n_used, xs, w_up, b_up, w_down, b_down):
    n_exp, d, two_ff = w_up.shape
    d_ff = w_down.shape[1]
    tm = EXPERT_TILE
    n_blocks = xs.shape[0] // (tm * ROW_SUBLANES)
    blk = lambda i, be, nu: (jnp.minimum(i, nu[0] - 1), 0)
    return pl.pallas_call(
        _expert_kernel,
        grid_spec=pltpu.PrefetchScalarGridSpec(
            num_scalar_prefetch=2, grid=(n_blocks,),
            in_specs=[pl.BlockSpec((tm * ROW_SUBLANES, LANES), blk),
                      pl.BlockSpec((None, d, two_ff), lambda i, be, nu: (be[i], 0, 0)),
                      pl.BlockSpec((None, 1, two_ff), lambda i, be, nu: (be[i], 0, 0)),
                      pl.BlockSpec((None, d_ff, d), lambda i, be, nu: (be[i], 0, 0)),
                      pl.BlockSpec((None, 1, d), lambda i, be, nu: (be[i], 0, 0))],
            out_specs=pl.BlockSpec((tm * ROW_SUBLANES, LANES), lambda i, be, nu: (i, 0)),
            scratch_shapes=[pltpu.VMEM((d, two_ff), BF16), pltpu.VMEM((d_ff, d), BF16)]),
        out_shape=jax.ShapeDtypeStruct(xs.shape, F32),
        compiler_params=_params(), name="experts")(
            block_e, n_used, xs, w_up, b_up.reshape(n_exp, 1, two_ff), w_down, b_down.reshape(n_exp, 1, d))


def _final_kernel(dest_ref, x1_ref, gate_ref, p_ref, ys_ref, gple_ref, wg_ref, bg_ref, wp_ref, gfin_ref,
                  y_ref, ybuf_ref, sem, *, tm, n_steps):
    i = pl.program_id(0)
    slot = i % 2

    def issue(step, to_slot):
        def body(t, carry):
            for k in range(TOP_K):
                row = dest_ref[(step * tm + t) * TOP_K + k]
                _tile_copy(ys_ref, row, ybuf_ref.at[to_slot, k], t, sem.at[to_slot]).start()
            return carry
        lax.fori_loop(0, tm, body, 0, unroll=8)

    @pl.when(i == 0)
    def _():
        issue(0, 0)

    @pl.when(i + 1 < n_steps)
    def _():
        issue(i + 1, 1 - slot)

    def drain(t, carry):
        for k in range(TOP_K):
            _tile_copy(ys_ref, 0, ybuf_ref.at[slot, k], t, sem.at[slot]).wait()
        return carry

    lax.fori_loop(0, tm, drain, 0, unroll=8)

    gate = gate_ref[...]
    pieces = []
    for s in range(ROW_SUBLANES):
        acc = gate[:, 0:1] * ybuf_ref[slot, 0, pl.ds(s, tm, stride=ROW_SUBLANES), :]
        for k in range(1, TOP_K):
            acc = acc + gate[:, k:k + 1] * ybuf_ref[slot, k, pl.ds(s, tm, stride=ROW_SUBLANES), :]
        pieces.append(acc)
    moe = jnp.concatenate(pieces, axis=1)
    x2 = x1_ref[...] + moe
    xn3 = _rms(x2, gple_ref[...])
    g = _sigmoid(_dot(xn3.astype(BF16), wg_ref[...]) + bg_ref[...])
    x3 = x2 + g * _dot(p_ref[...].astype(BF16), wp_ref[...])
    y_ref[...] = _rms(x3, gfin_ref[...])


def _final_call(dest, x1, gate, p, ys, w):
    t, d = x1.shape
    tm = ROW_TILE
    row = lambda i, dst: (i, 0)
    full = lambda shape: pl.BlockSpec(shape, lambda i, dst: (0,) * len(shape))
    body = functools.partial(_final_kernel, tm=tm, n_steps=t // tm)
    return pl.pallas_call(
        body,
        grid_spec=pltpu.PrefetchScalarGridSpec(
            num_scalar_prefetch=1, grid=(t // tm,),
            in_specs=[pl.BlockSpec((tm, d), row), pl.BlockSpec((tm, LANES), row),
                      pl.BlockSpec((tm, p.shape[1]), row), pl.BlockSpec(memory_space=pl.ANY),
                      full(w["g_ple"].shape), full(w["w_ple_gate"].shape), full(w["b_ple_gate"].shape),
                      full(w["w_ple"].shape), full(w["g_final"].shape)],
            out_specs=pl.BlockSpec((tm, d), row),
            scratch_shapes=[pltpu.VMEM((2, TOP_K, tm * ROW_SUBLANES, LANES), F32),
                            pltpu.SemaphoreType.DMA((2,))]),
        out_shape=jax.ShapeDtypeStruct((t, d), F32),
        compiler_params=_params(), name="final")(
            dest, x1, gate, p, ys, w["g_ple"], w["w_ple_gate"], w["b_ple_gate"], w["w_ple"], w["g_final"])


def _rot_cols(w):
    half = QK_ROPE_DIM // 2
    return jnp.concatenate([-w[:, half:], w[:, :half]], axis=1)


def _rope_tables(pos):
    half = QK_ROPE_DIM // 2
    inv = ROPE_THETA ** (-jnp.arange(half, dtype=F32) / half)
    ang = pos.astype(F32)[:, None] * inv[None, :]
    zeros = jnp.zeros((pos.shape[0], LANES - QK_ROPE_DIM), F32)
    cos, sin = jnp.cos(ang), jnp.sin(ang)
    return jnp.concatenate([cos, cos, zeros], axis=1), jnp.concatenate([sin, sin, zeros], axis=1)


def _prep_weights(g_attn, w_in, w_pool, pool_scale, g_q, w_uq, g_kv, w_uk, w_uv, w_o, g_ffn, w_router,
                  b_router, g_ple, w_ple_gate, b_ple_gate, w_ple, g_final):
    d = w_in.shape[0]
    o3 = POOL_WIDTH + Q_LORA + KV_LORA
    w_kpe = w_in[:, o3:o3 + QK_ROPE_DIM]
    zpad = jnp.zeros((d, LANES - QK_ROPE_DIM), F32)
    w_in_ext = jnp.concatenate([w_in, zpad, _rot_cols(w_kpe), zpad], axis=1)
    zq = jnp.zeros((Q_LORA, LANES - QK_ROPE_DIM), F32)
    mains, rots = [], []
    for h in range(N_HEADS):
        mains += [w_uq[:, h, :], zq]
        rots += [_rot_cols(w_uq[:, h, QK_NOPE_DIM:]), zq]
    w_q = jnp.concatenate(mains + rots, axis=1)
    row = lambda v: v.reshape(1, -1).astype(F32)
    n_pad = LANES - N_EXPERTS
    return {
        "g_attn": row(g_attn), "w_in": w_in_ext.astype(BF16), "w_pool": w_pool.astype(BF16),
        "pool_scale": row(pool_scale), "g_q": row(g_q), "w_q": w_q.astype(BF16), "g_kv": row(g_kv),
        "w_k": w_uk.reshape(KV_LORA, N_HEADS * QK_NOPE_DIM).astype(BF16),
        "w_v": w_uv.reshape(KV_LORA, N_HEADS * V_HEAD_DIM).T.astype(BF16),
        "w_ukt": jnp.transpose(w_uk, (1, 2, 0)).astype(BF16),
        "w_uv_h": jnp.transpose(w_uv, (1, 0, 2)).astype(BF16),
        "w_o": w_o.astype(BF16), "g_ffn": row(g_ffn),
        "w_router": jnp.pad(w_router.astype(F32), ((0, 0), (0, n_pad))),
        "b_router": jnp.concatenate([b_router.astype(F32), jnp.full((n_pad,), NEG_BIG, F32)]).reshape(1, LANES),
        "g_ple": row(g_ple), "w_ple_gate": w_ple_gate.astype(BF16), "b_ple_gate": row(b_ple_gate),
        "w_ple": w_ple.astype(BF16), "g_final": row(g_final),
    }


def kernel(x_prompt, x_sample, cache_ckv, cache_kpe, state_pool, p_prompt, p_sample, g_attn, w_in, w_pool, pool_scale, g_q, w_uq, g_kv, w_uk, w_uv, w_o, g_ffn, w_router, b_router, w_up, b_up, w_down, b_down, g_ple, w_ple_gate, b_ple_gate, w_ple, g_final):
    assert g_attn.shape[0] == 1, "single trunk layer"
    batch, seq, d = x_prompt.shape
    n_dec, dec_seq, _ = x_sample.shape
    past = cache_ckv.shape[2]
    assert past % CHUNK == 0 and dec_seq <= CHUNK and past >= POOL_STATE
    assert seq % ATTN_TILE == 0 and seq % TOKEN_TILE == 0 and TOKEN_TILE % dec_seq == 0
    assert dec_seq >= POOL_STATE and ATTN_TILE % CHUNK == 0
    assert TOKEN_TILE == ATTN_TILE
    tp, ts = batch * seq, n_dec * dec_seq
    assert ts % TOKEN_TILE == 0 and tp % ROW_TILE == 0 and ts % ROW_TILE == 0

    w = _prep_weights(g_attn[0], w_in[0], w_pool[0], pool_scale[0], g_q[0], w_uq[0], g_kv[0], w_uk[0], w_uv[0],
                      w_o[0], g_ffn[0], w_router[0], b_router[0], g_ple[0], w_ple_gate[0], b_ple_gate[0],
                      w_ple[0], g_final)
    cs_p, sn_p = _rope_tables(jnp.arange(seq))
    cs_s, sn_s = _rope_tables(past + jnp.tile(jnp.arange(dec_seq), TOKEN_TILE // dec_seq))

    xp = x_prompt.reshape(tp, d)
    xs = x_sample.reshape(ts, d)

    ckv_p, kpe_p, py_p, pst_p, q_p, k_p, v_p = _pre_call(xp, cs_p, sn_p, None, w, absorbed=False, seg_len=seq, past=0)
    at_p = _attn_prompt_call(q_p, k_p, v_p, batch, seq)
    ckv_s, kpe_s, py_s, pst_s, qa_s, qp_s = _pre_call(xs, cs_s, sn_s, state_pool[0], w, absorbed=True,
                                                       seg_len=dec_seq, past=past)
    at_s = _attn_sample_call(qa_s, qp_s, cache_ckv[0], cache_kpe[0], ckv_s, kpe_s, w["w_uv_h"], n_dec, dec_seq)

    cnt0 = jnp.zeros((8, LANES), F32)
    x1_p, xn_p, meta_p, gate_p, cnt1 = _post_call(xp, py_p, at_p, cnt0, w)
    x1_s, xn_s, meta_s, gate_s, cnt2 = _post_call(xs, py_s, at_s, cnt1, w)

    te = EXPERT_TILE
    counts = cnt2[0, :N_EXPERTS].astype(jnp.int32)
    padded = (counts + te - 1) // te * te
    pad_end = jnp.cumsum(padded)
    pad_start = pad_end - padded
    n_blocks = (tp + ts) * TOP_K // te + N_EXPERTS
    block_start = jnp.arange(n_blocks, dtype=jnp.int32) * te
    block_e = jnp.minimum(jnp.sum(pad_end[None, :] <= block_start[:, None], axis=1), N_EXPERTS - 1).astype(jnp.int32)
    n_used = (pad_end[-1:] // te).astype(jnp.int32)
    dest_p = (pad_start[meta_p[:, TOP_K:2 * TOP_K]] + meta_p[:, :TOP_K]).reshape(-1)
    dest_s = (pad_start[meta_s[:, TOP_K:2 * TOP_K]] + meta_s[:, :TOP_K]).reshape(-1)

    assert d == ROW_SUBLANES * LANES
    x_sorted = jnp.zeros((n_blocks * te * ROW_SUBLANES, LANES), F32)
    x_sorted = _scatter_call(dest_p, xn_p, x_sorted)
    x_sorted = _scatter_call(dest_s, xn_s, x_sorted)
    y_sorted = _expert_call(block_e, n_used, x_sorted, w_up[0], b_up[0], w_down[0], b_down[0])

    y_p = _final_call(dest_p, x1_p, gate_p, p_prompt[0].reshape(tp, -1), y_sorted, w)
    y_s = _final_call(dest_s, x1_s, gate_s, p_sample[0].reshape(ts, -1), y_sorted, w)

    return (y_p.reshape(batch, seq, d), y_s.reshape(n_dec, dec_seq, d),
            ckv_p.reshape(1, batch, seq, KV_LORA), kpe_p.reshape(1, batch, seq, QK_ROPE_DIM),
            pst_p.reshape(1, batch, POOL_STATE, POOL_WIDTH),
            ckv_s.reshape(1, n_dec, dec_seq, KV_LORA), kpe_s.reshape(1, n_dec, dec_seq, QK_ROPE_DIM),
            pst_s.reshape(1, n_dec, POOL_STATE, POOL_WIDTH))
```
